```python
import math
import jax
import jax.numpy as jnp
from jax import lax
import numpy as np

D_MODEL = 2048
BATCH = 2
SEQ = 4096
DEPTH = 2
DEC_BATCH = 16
DEC_SEQ = 16
PAST_LEN = 1024

CHUNK = 64
N_META = 16
N_A_LAYERS = DEPTH // 2
N_B_LAYERS = DEPTH - N_A_LAYERS
HGRN_EXPAND = 128
HGRN_HEADS = D_MODEL // HGRN_EXPAND
HGRN_DK = HGRN_EXPAND
HGRN_DV = D_MODEL // HGRN_HEADS
HEAD_DIM = 64
N_Q_HEADS = D_MODEL // HEAD_DIM
N_KV_HEADS = 8
GROUP = N_Q_HEADS // N_KV_HEADS
WINDOW = 128
WINDOW_CHUNKS = WINDOW // CHUNK
ATTN_SCALE = HEAD_DIM ** -0.5
NUM_BUCKETS = 32
MAX_DISTANCE = 128
N_EXPERTS = 32
TOP_K = 4
D_EXPERT = D_MODEL
SWIGLU_ALPHA = 1.702
SWIGLU_LIMIT = 7.0
DEEPNORM_ALPHA = (2 * DEPTH) ** 0.25
DEEPNORM_BETA = (8 * DEPTH) ** -0.25
LN_EPS = 1e-5
RMS_EPS = 1e-6
F32 = jnp.float32

kernel_name = 'yoco_hgrn2_swa_sink_moe_stream_step'


def layer_norm(x, g, b):
    xf = x.astype(F32)
    mu = jnp.mean(xf, axis=-1, keepdims=True)
    var = jnp.mean(jnp.square(xf - mu), axis=-1, keepdims=True)
    return ((xf - mu) * lax.rsqrt(var + LN_EPS) * g.astype(F32) + b.astype(F32)).astype(x.dtype)


def hgrn2_gates(h, w_in, lb):
    bsz, length = h.shape[:2]
    q, f_logit, inp, g = jnp.split(h @ w_in, 4, axis=-1)
    f = lb + (1.0 - lb) * jax.nn.sigmoid(f_logit.astype(F32))
    heads = lambda a: a.reshape(bsz, length, HGRN_HEADS, -1)
    return (heads(jax.nn.silu(q.astype(F32))), heads(1.0 - f), heads(inp.astype(F32)),
            heads(jnp.log(f)), g)


def hgrn2_recurrence(q, k, v, log_f, s0, block):
    bsz, length = q.shape[:2]
    n = length // block
    blocks = lambda a: jnp.swapaxes(a.reshape(bsz, n, block, HGRN_HEADS, a.shape[-1]), 0, 1)
    causal = jnp.tril(jnp.ones((block, block), dtype=bool))[None, :, :, None, None]

    def step(s, blk):
        qb, kb, vb, gb = blk
        cum = jnp.cumsum(gb, axis=1)
        o_prev = jnp.einsum('bthk,bhkv->bthv', qb * jnp.exp(cum), s)
        decay = jnp.exp(jnp.where(causal, cum[:, :, None] - cum[:, None, :], -jnp.inf))
        scores = jnp.einsum('bthk,bshk,btshk->bhts', qb, kb, decay)
        o_blk = jnp.einsum('bhts,bshv->bthv', scores, vb)
        last = cum[:, -1]
        s_new = jnp.exp(last)[..., None] * s + jnp.einsum(
            'bshk,bshv->bhkv', kb * jnp.exp(last[:, None] - cum), vb)
        return s_new, o_prev + o_blk

    s_fin, o = lax.scan(step, s0.astype(F32), (blocks(q), blocks(k), blocks(v), blocks(log_f)))
    return jnp.swapaxes(o, 0, 1).reshape(bsz, length, HGRN_HEADS, HGRN_DV), s_fin


def hgrn2_readout(o, g, gnorm, w_out):
    bsz, length = o.shape[:2]
    o = o * lax.rsqrt(jnp.mean(jnp.square(o), axis=-1, keepdims=True) + RMS_EPS)
    o = o * gnorm.astype(F32).reshape(HGRN_HEADS, HGRN_DV)
    o = o.reshape(bsz, length, D_MODEL) * jax.nn.silu(g.astype(F32))
    return o.astype(g.dtype) @ w_out


def t5_bucket(rel):
    nb = NUM_BUCKETS // 2
    max_exact = nb // 2
    ret = jnp.where(rel > 0, nb, 0)
    n = jnp.abs(rel)
    large = max_exact + (jnp.log(jnp.maximum(n, 1).astype(F32) / max_exact)
                         / math.log(MAX_DISTANCE / max_exact) * (nb - max_exact)).astype(jnp.int32)
    large = jnp.minimum(large, nb - 1)
    return ret + jnp.where(n < max_exact, n, large)


def rel_bias_lookup(table, q_pos, k_pos):
    bucket = t5_bucket(k_pos[..., None, :] - q_pos[..., :, None])
    return jnp.moveaxis(table[bucket], -1, -3)


def sink_attention(q, k, v, bias, valid, sinks):
    bsz, nblk, nq = q.shape[:3]
    nk = k.shape[2]
    qg = q.reshape(bsz, nblk, nq, N_KV_HEADS, GROUP, HEAD_DIM)
    logits = jnp.einsum('bnqhgd,bnkhd->bnhgqk', qg, k).astype(F32) * ATTN_SCALE
    logits = logits + bias.reshape(nblk, N_KV_HEADS, GROUP, nq, nk)[None].astype(F32)
    logits = jnp.where(valid[None, :, None, None], logits, -jnp.inf)
    sink = jnp.broadcast_to(sinks.astype(F32).reshape(N_KV_HEADS, GROUP, 1, 1), logits.shape[:-1] + (1,))
    probs = jax.nn.softmax(jnp.concatenate([logits, sink], axis=-1), axis=-1)[..., :-1]
    out = jnp.einsum('bnhgqk,bnkhd->bnqhgd', probs.astype(v.dtype), v)
    return out.reshape(bsz, nblk, nq, N_Q_HEADS * HEAD_DIM)


def swa_prompt(q, k, v, sinks, table):
    bsz = q.shape[0]
    nc = (q.shape[1] - N_META) // CHUNK
    pos_m = jnp.arange(N_META)[None]
    out_m = sink_attention(q[:, None, :N_META], k[:, None, :N_META], v[:, None, :N_META],
                           rel_bias_lookup(table, pos_m, pos_m),
                           jnp.ones((1, N_META, N_META), dtype=bool), sinks)

    def band(a):
        pad = jnp.pad(a[:, N_META:], ((0, 0), (WINDOW_CHUNKS * CHUNK, 0), (0, 0), (0, 0)))
        pad = pad.reshape(bsz, nc + WINDOW_CHUNKS, CHUNK, N_KV_HEADS, HEAD_DIM)
        rows = jnp.concatenate([pad[:, w:w + nc] for w in range(WINDOW_CHUNKS + 1)], axis=2)
        meta = jnp.broadcast_to(a[:, None, :N_META], (bsz, nc, N_META, N_KV_HEADS, HEAD_DIM))
        return jnp.concatenate([meta, rows], axis=2)

    start = jnp.arange(nc) * CHUNK
    q_frame = start[:, None] + jnp.arange(CHUNK)[None]
    k_frame = start[:, None] - WINDOW_CHUNKS * CHUNK + jnp.arange((WINDOW_CHUNKS + 1) * CHUNK)[None]
    k_pos = jnp.concatenate([jnp.broadcast_to(jnp.arange(N_META), (nc, N_META)), N_META + k_frame], axis=1)
    valid = jnp.concatenate([jnp.ones((nc, N_META), dtype=bool), k_frame >= 0], axis=1)
    valid = jnp.broadcast_to(valid[:, None], (nc, CHUNK, valid.shape[-1]))
    bias = rel_bias_lookup(table, N_META + q_frame, k_pos)
    qr = q[:, N_META:].reshape(bsz, nc, CHUNK, N_Q_HEADS, HEAD_DIM)
    out_r = sink_attention(qr, band(k), band(v), bias, valid, sinks)
    return jnp.concatenate([out_m.reshape(bsz, N_META, -1), out_r.reshape(bsz, nc * CHUNK, -1)], axis=1)


def swa_sample(q, k_new, v_new, meta_k, meta_v, win_k, win_v, sinks, table):
    ds = q.shape[1]
    win = win_k.shape[1]
    keys = jnp.concatenate([meta_k, win_k, k_new], axis=1)[:, None]
    vals = jnp.concatenate([meta_v, win_v, v_new], axis=1)[:, None]
    q_pos = N_META + PAST_LEN + jnp.arange(ds)
    k_pos = jnp.concatenate([jnp.arange(N_META), N_META + PAST_LEN - win + jnp.arange(win), q_pos])
    bias = rel_bias_lookup(table, q_pos[None], k_pos[None])
    valid = jnp.ones((1, ds, k_pos.shape[0]), dtype=bool)
    return sink_attention(q[:, None], keys, vals, bias, valid, sinks)[:, 0]


def shared_kv(h, w, b):
    bsz, length = h.shape[:2]
    k, v = jnp.split(h @ w + b, 2, axis=-1)
    return (k.reshape(bsz, length, N_KV_HEADS, HEAD_DIM), v.reshape(bsz, length, N_KV_HEADS, HEAD_DIM))


def clamped_swiglu(hcat):
    glu = jnp.minimum(hcat[..., ::2], SWIGLU_LIMIT)
    lin = jnp.clip(hcat[..., 1::2], -SWIGLU_LIMIT, SWIGLU_LIMIT)
    return glu * jax.nn.sigmoid(SWIGLU_ALPHA * glu) * (lin + 1.0)


def moe_ffn(x, w_r, b_r, w1, b1, w2, b2):
    shp = x.shape
    xt = x.reshape(-1, D_MODEL)
    logits = (xt @ w_r + b_r).astype(F32)
    top_val, top_idx = lax.top_k(logits, TOP_K)
    top_w = jax.nn.softmax(top_val, axis=-1)
    gates = jnp.einsum('tk,tke->et', top_w, jax.nn.one_hot(top_idx, N_EXPERTS, dtype=F32))

    def expert(acc, p):
        w1e, b1e, w2e, b2e, ge = p
        y = clamped_swiglu(xt @ w1e + b1e) @ w2e + b2e
        return acc + ge[:, None] * y.astype(F32), None

    acc, _ = lax.scan(expert, jnp.zeros(xt.shape, F32), (w1, b1, w2, b2, gates))
    return acc.astype(x.dtype).reshape(shp)


def setup_inputs(seed: int = 0) -> dict:
    key = jax.random.key(seed)
    ks = iter(jax.random.split(key, 40))
    nrm = lambda shape, scale: jax.random.normal(next(ks), shape, F32) * scale
    win = min(WINDOW, PAST_LEN)
    kv_width = N_KV_HEADS * HEAD_DIM
    q_width = N_Q_HEADS * HEAD_DIM
    return {
        'x_prompt': nrm((BATCH, SEQ, D_MODEL), 1.0),
        'x_sample': nrm((DEC_BATCH, DEC_SEQ, D_MODEL), 1.0),
        'state_hgrn': nrm((N_A_LAYERS, DEC_BATCH, HGRN_HEADS, HGRN_DK, HGRN_DV), 0.3),
        'cache_meta_k': nrm((DEC_BATCH, N_META, N_KV_HEADS, HEAD_DIM), 1.0),
        'cache_meta_v': nrm((DEC_BATCH, N_META, N_KV_HEADS, HEAD_DIM), 1.0),
        'cache_win_k': nrm((DEC_BATCH, win, N_KV_HEADS, HEAD_DIM), 1.0),
        'cache_win_v': nrm((DEC_BATCH, win, N_KV_HEADS, HEAD_DIM), 1.0),
        'meta_tokens': nrm((N_META, D_MODEL), 1.0),
        'a_w_in': nrm((N_A_LAYERS, D_MODEL, 4 * D_MODEL), D_MODEL ** -0.5),
        'a_lb_logits': nrm((N_A_LAYERS + 1, HGRN_HEADS * HGRN_DK), 0.5),
        'a_gnorm': 1.0 + nrm((N_A_LAYERS, D_MODEL), 0.02),
        'a_w_out': nrm((N_A_LAYERS, D_MODEL, D_MODEL), D_MODEL ** -0.5 * DEEPNORM_BETA),
        'kv_w': nrm((D_MODEL, 2 * kv_width), D_MODEL ** -0.5),
        'kv_b': nrm((2 * kv_width,), 0.02),
        'b_w_q': nrm((N_B_LAYERS, D_MODEL, q_width), D_MODEL ** -0.5),
        'b_b_q': nrm((N_B_LAYERS, q_width), 0.02),
        'b_sinks': nrm((N_B_LAYERS, N_Q_HEADS), 0.5),
        'b_w_o': nrm((N_B_LAYERS, q_width, D_MODEL), q_width ** -0.5 * DEEPNORM_BETA),
        'b_b_o': nrm((N_B_LAYERS, D_MODEL), 0.02),
        'rel_bias': nrm((NUM_BUCKETS, N_Q_HEADS), 0.5),
        'ln1_g': 1.0 + nrm((DEPTH, D_MODEL), 0.02),
        'ln1_b': nrm((DEPTH, D_MODEL), 0.02),
        'ln2_g': 1.0 + nrm((DEPTH, D_MODEL), 0.02),
        'ln2_b': nrm((DEPTH, D_MODEL), 0.02),
        'moe_w_r': nrm((DEPTH, D_MODEL, N_EXPERTS), D_MODEL ** -0.5),
        'moe_b_r': nrm((DEPTH, N_EXPERTS), 0.01),
        'moe_w1': nrm((DEPTH, N_EXPERTS, D_MODEL, 2 * D_EXPERT), D_MODEL ** -0.5),
        'moe_b1': nrm((DEPTH, N_EXPERTS, 2 * D_EXPERT), 0.02),
        'moe_w2': nrm((DEPTH, N_EXPERTS, D_EXPERT, D_MODEL), D_EXPERT ** -0.5 * DEEPNORM_BETA),
        'moe_b2': nrm((DEPTH, N_EXPERTS, D_MODEL), 0.02),
    }


def reference(x_prompt, x_sample, state_hgrn, cache_meta_k, cache_meta_v, cache_win_k, cache_win_v,
              meta_tokens, a_w_in, a_lb_logits, a_gnorm, a_w_out, kv_w, kv_b,
              b_w_q, b_b_q, b_sinks, b_w_o, b_b_o, rel_bias,
              ln1_g, ln1_b, ln2_g, ln2_b, moe_w_r, moe_b_r, moe_w1, moe_b1, moe_w2, moe_b2):
    dt = x_prompt.dtype
    bp = x_prompt.shape[0]
    lower_bounds = jnp.cumsum(jax.nn.softmax(a_lb_logits.astype(F32), axis=0), axis=0)
    h_p = jnp.concatenate(
        [jnp.broadcast_to(meta_tokens.astype(dt)[None], (bp, N_META, D_MODEL)), x_prompt], axis=1)
    h_s = x_sample
    pre = lambda a: a[:, :N_META]
    post = lambda a: a[:, N_META:]
    hgrn_p, hgrn_s = [], []
    for layer in range(DEPTH):
        if layer < N_A_LAYERS:
            lb = lower_bounds[layer]
            q, k, v, lf, g = hgrn2_gates(h_p, a_w_in[layer], lb)
            s_init = jnp.zeros((bp, HGRN_HEADS, HGRN_DK, HGRN_DV), F32)
            o_m, s_m = hgrn2_recurrence(pre(q), pre(k), pre(v), pre(lf), s_init, N_META)
            o_r, s_p = hgrn2_recurrence(post(q), post(k), post(v), post(lf), s_m, CHUNK)
            mix_p = hgrn2_readout(jnp.concatenate([o_m, o_r], axis=1), g, a_gnorm[layer], a_w_out[layer])
            q, k, v, lf, g = hgrn2_gates(h_s, a_w_in[layer], lb)
            o_s, s_s = hgrn2_recurrence(q, k, v, lf, state_hgrn[layer], h_s.shape[1])
            mix_s = hgrn2_readout(o_s, g, a_gnorm[layer], a_w_out[layer])
            hgrn_p.append(s_p)
            hgrn_s.append(s_s)
        else:
            j = layer - N_A_LAYERS
            if j == 0:
                k_p, v_p = shared_kv(h_p, kv_w, kv_b)
                k_s, v_s = shared_kv(h_s, kv_w, kv_b)
            q_p = (h_p @ b_w_q[j] + b_b_q[j]).reshape(bp, h_p.shape[1], N_Q_HEADS, HEAD_DIM)
            mix_p = swa_prompt(q_p, k_p, v_p, b_sinks[j], rel_bias) @ b_w_o[j] + b_b_o[j]
            q_s = (h_s @ b_w_q[j] + b_b_q[j]).reshape(h_s.shape[0], h_s.shape[1], N_Q_HEADS, HEAD_DIM)
            mix_s = swa_sample(q_s, k_s, v_s, cache_meta_k, cache_meta_v, cache_win_k, cache_win_v,
                               b_sinks[j], rel_bias) @ b_w_o[j] + b_b_o[j]
        h_p = layer_norm(DEEPNORM_ALPHA * h_p + mix_p, ln1_g[layer], ln1_b[layer])
        h_s = layer_norm(DEEPNORM_ALPHA * h_s + mix_s, ln1_g[layer], ln1_b[layer])
        moe_args = (moe_w_r[layer], moe_b_r[layer], moe_w1[layer], moe_b1[layer], moe_w2[layer], moe_b2[layer])
        h_p = layer_norm(DEEPNORM_ALPHA * h_p + moe_ffn(h_p, *moe_args), ln2_g[layer], ln2_b[layer])
        h_s = layer_norm(DEEPNORM_ALPHA * h_s + moe_ffn(h_s, *moe_args), ln2_g[layer], ln2_b[layer])
    win_p = min(WINDOW, k_p.shape[1] - N_META)
    n_p = k_p.shape[1]
    return (h_p[:, N_META:], h_s, jnp.stack(hgrn_p), k_p[:, :N_META], v_p[:, :N_META],
            k_p[:, n_p - win_p:], v_p[:, n_p - win_p:], jnp.stack(hgrn_s), k_s, v_s)
```

```python
import functools
import math

import jax
import jax.numpy as jnp
from jax import lax
from jax.experimental import pallas as pl
from jax.experimental.pallas import tpu as pltpu

F32 = jnp.float32
BF16 = jnp.bfloat16
I32 = jnp.int32

D_MODEL = 2048
DEPTH = 2
CHUNK = 64
N_META = 16
HGRN_HEADS = 16
HGRN_DK = 128
HEAD_DIM = 64
N_Q_HEADS = 32
N_KV_HEADS = 8
GROUP = N_Q_HEADS // N_KV_HEADS
KV_WIDTH = N_KV_HEADS * HEAD_DIM
WINDOW_CHUNKS = 2
PAST_LEN = 1024
ATTN_SCALE = HEAD_DIM ** -0.5
NUM_BUCKETS = 32
MAX_DISTANCE = 128
N_EXPERTS = 32
TOP_K = 4
SWIGLU_ALPHA = 1.702
SWIGLU_LIMIT = 7.0
DEEPNORM_ALPHA = (2 * DEPTH) ** 0.25
LN_EPS = 1e-5
RMS_EPS = 1e-6

LANES = 128
SUBLANES = 8
NEG_INF = float("-inf")


def _params(semantics, vmem_mb=None):
    return pltpu.CompilerParams(
        dimension_semantics=semantics,
        vmem_limit_bytes=None if vmem_mb is None else vmem_mb * 2 ** 20)


def _sigmoid(x):
    return 1.0 / (1.0 + jnp.exp(-x))


def _layer_norm_rows(z, g, b):
    mu = jnp.mean(z, axis=-1, keepdims=True)
    zc = z - mu
    var = jnp.mean(zc * zc, axis=-1, keepdims=True)
    return zc * lax.rsqrt(var + LN_EPS) * g + b


def _mm_body(*refs, nk, has_bias, has_ln):
    refs = list(refs)
    x_ref, w_ref = refs[0], refs[1]
    pos = 2
    b_ref = None
    if has_bias:
        b_ref = refs[pos]
        pos += 1
    if has_ln:
        res_ref, g_ref, beta_ref = refs[pos:pos + 3]
        pos += 3
    o_ref, acc_ref = refs[pos], refs[pos + 1]
    k = pl.program_id(2)
    part = jnp.dot(x_ref[...].astype(BF16), w_ref[...].astype(BF16),
                   preferred_element_type=F32)

    @pl.when(k == 0)
    def _():
        acc_ref[...] = part

    @pl.when(k > 0)
    def _():
        acc_ref[...] += part

    @pl.when(k == nk - 1)
    def _():
        y = acc_ref[...]
        if has_bias:
            y = y + b_ref[...]
        if has_ln:
            y = _layer_norm_rows(DEEPNORM_ALPHA * res_ref[...] + y, g_ref[...], beta_ref[...])
        o_ref[...] = y.astype(o_ref.dtype)


def _matmul(x, w, *, bias=None, ln=None, tm, tn, tk, vmem_mb=48):
    m, kdim = x.shape
    n = w.shape[-1]
    assert m % tm == 0 and n % tn == 0 and kdim % tk == 0
    assert ln is None or tn == n
    nk = kdim // tk
    in_specs = [pl.BlockSpec((tm, tk), lambda i, j, k: (i, k)),
                pl.BlockSpec((tk, tn), lambda i, j, k: (k, j))]
    args = [x, w]
    if bias is not None:
        in_specs.append(pl.BlockSpec((1, tn), lambda i, j, k: (0, j)))
        args.append(bias.reshape(1, n))
    if ln is not None:
        res, g, beta = ln
        in_specs += [pl.BlockSpec((tm, tn), lambda i, j, k: (i, j)),
                     pl.BlockSpec((1, tn), lambda i, j, k: (0, j)),
                     pl.BlockSpec((1, tn), lambda i, j, k: (0, j))]
        args += [res, g.reshape(1, n), beta.reshape(1, n)]
    return pl.pallas_call(
        functools.partial(_mm_body, nk=nk, has_bias=bias is not None, has_ln=ln is not None),
        out_shape=jax.ShapeDtypeStruct((m, n), F32),
        grid=(m // tm, n // tn, nk),
        in_specs=in_specs,
        out_specs=pl.BlockSpec((tm, tn), lambda i, j, k: (i, j)),
        scratch_shapes=[pltpu.VMEM((tm, tn), F32)],
        compiler_params=_params(("parallel", "parallel", "arbitrary"), vmem_mb),
    )(*args)


def _hgrn_chunk(qr, fl, v, g, lb, gn, st, cum_scr, k_scr, v_scr, o_scr, c):
    f = lb + (1.0 - lb) * _sigmoid(fl)
    q = qr * _sigmoid(qr)
    k = 1.0 - f
    row = lax.broadcasted_iota(I32, (c, c), 0)
    col = lax.broadcasted_iota(I32, (c, c), 1)
    tri = (row >= col).astype(F32)
    cum = jnp.dot(tri, jnp.log(f), precision=lax.Precision.HIGHEST, preferred_element_type=F32)
    last = cum[c - 1:c, :]
    o_scr[0:c, :] = lax.dot_general((q * jnp.exp(cum)).astype(BF16), st.astype(BF16),
                                    (((1,), (1,)), ((), ())), preferred_element_type=F32)
    cum_scr[0:c, :] = cum
    k_scr[0:c, :] = k
    v_scr[0:c, :] = v
    for sb in range(c // SUBLANES):
        r0 = sb * SUBLANES
        q_t = q[r0:, :]
        cum_t = cum[r0:, :]
        t_idx = lax.broadcasted_iota(I32, (c - r0, 1), 0) + r0

        def inner(j, acc, r0=r0, q_t=q_t, cum_t=cum_t, t_idx=t_idx):
            s = r0 + j
            cs = cum_scr[pl.ds(s, 1), :]
            ks = k_scr[pl.ds(s, 1), :]
            vs = v_scr[pl.ds(s, 1), :]
            d = jnp.where(t_idx >= s, cum_t - cs, NEG_INF)
            wcol = jnp.sum(q_t * ks * jnp.exp(d), axis=-1, keepdims=True)
            return acc + wcol * vs

        acc = lax.fori_loop(0, SUBLANES, inner, jnp.zeros((c - r0, LANES), F32))
        o_scr[r0:c, :] += acc
    kd = k * jnp.exp(last - cum)
    st_new = st * jnp.exp(last) + lax.dot_general(
        v.astype(BF16), kd.astype(BF16), (((0,), (0,)), ((), ())), preferred_element_type=F32)
    o = o_scr[0:c, :]
    o = o * lax.rsqrt(jnp.mean(o * o, axis=-1, keepdims=True) + RMS_EPS)
    o = o * gn * (g * _sigmoid(g))
    return o, st_new


def _hgrn_body(q_ref, f_ref, v_ref, g_ref, lb_ref, gn_ref, s0_ref, o_ref, s_out_ref,
               st_ref, cum_scr, k_scr, v_scr, o_scr, *, c, cps, hb, n_steps):
    step = pl.program_id(2)

    @pl.when(step == 0)
    def _():
        for hh in range(hb):
            st_ref[hh] = s0_ref[0, hh].T

    for hh in range(hb):
        lanes = slice(hh * LANES, (hh + 1) * LANES)
        lb = lb_ref[:, lanes]
        gn = gn_ref[:, lanes]

        def chunk(ci, carry, hh=hh, lanes=lanes, lb=lb, gn=gn):
            rows = pl.ds(pl.multiple_of(ci * c, c), c)
            o, st_new = _hgrn_chunk(q_ref[rows, lanes], f_ref[rows, lanes], v_ref[rows, lanes],
                                    g_ref[rows, lanes], lb, gn, st_ref[hh],
                                    cum_scr, k_scr, v_scr, o_scr, c)
            o_ref[rows, lanes] = o
            st_ref[hh] = st_new
            return carry

        lax.fori_loop(0, cps, chunk, 0)

    @pl.when(step == n_steps - 1)
    def _():
        for hh in range(hb):
            s_out_ref[0, hh] = st_ref[hh].T


def _hgrn(proj, lb, gn, s0, *, n_seq, seq_len, row0, c, cps, hb):
    rows = c * cps
    assert seq_len % rows == 0 and row0 % rows == 0
    n_steps = seq_len // rows
    hgroups = HGRN_HEADS // hb
    wb = LANES * hb
    rb0 = row0 // rows

    def part(p):
        return pl.BlockSpec((rows, wb), lambda b, h, s, p=p: (rb0 + b * n_steps + s, p * hgroups + h))

    vec = pl.BlockSpec((1, wb), lambda b, h, s: (0, h))
    st_spec = pl.BlockSpec((1, hb, HGRN_DK, HGRN_DK), lambda b, h, s: (b, h, 0, 0))
    return pl.pallas_call(
        functools.partial(_hgrn_body, c=c, cps=cps, hb=hb, n_steps=n_steps),
        out_shape=(jax.ShapeDtypeStruct((n_seq * seq_len, D_MODEL), F32),
                   jax.ShapeDtypeStruct((n_seq, HGRN_HEADS, HGRN_DK, HGRN_DK), F32)),
        grid=(n_seq, hgroups, n_steps),
        in_specs=[part(0), part(1), part(2), part(3), vec, vec, st_spec],
        out_specs=(pl.BlockSpec((rows, wb), lambda b, h, s: (b * n_steps + s, h)), st_spec),
        scratch_shapes=[pltpu.VMEM((hb, HGRN_DK, HGRN_DK), F32)] + [pltpu.VMEM((c, LANES), F32)] * 4,
        compiler_params=_params(("parallel", "parallel", "arbitrary")),
    )(proj, proj, proj, proj, lb, gn, s0)


def _attn_body(*refs, n_src, nq, masked):
    sink_ref, q_ref = refs[0], refs[1]
    src = refs[2:2 + 3 * n_src]
    o_ref = refs[2 + 3 * n_src]
    chunk = pl.program_id(1)
    q_all = q_ref[...] * ATTN_SCALE
    pieces = [None] * N_Q_HEADS
    for h in range(N_KV_HEADS):
        heads = range(GROUP * h, GROUP * (h + 1))
        qs = jnp.concatenate([q_all[:, HEAD_DIM * a:HEAD_DIM * (a + 1)] for a in heads], axis=0).astype(BF16)
        sink = jnp.concatenate([jnp.full((nq, 1), sink_ref[a], F32) for a in heads], axis=0)
        hs = slice(HEAD_DIM * h, HEAD_DIM * (h + 1))
        logits = []
        m = sink
        for i in range(n_src):
            k_ref, _, b_ref = src[3 * i:3 * i + 3]
            lg = lax.dot_general(qs, k_ref[:, hs].astype(BF16), (((1,), (1,)), ((), ())),
                                 preferred_element_type=F32) + b_ref[h]
            if masked and i < WINDOW_CHUNKS:
                lg = jnp.where(chunk >= WINDOW_CHUNKS - i, lg, NEG_INF)
            logits.append(lg)
            m = jnp.maximum(m, jnp.max(lg, axis=-1, keepdims=True))
        den = jnp.exp(sink - m)
        acc = jnp.zeros((GROUP * nq, HEAD_DIM), F32)
        for i in range(n_src):
            v_ref = src[3 * i + 1]
            p = jnp.exp(logits[i] - m)
            den = den + jnp.sum(p, axis=-1, keepdims=True)
            acc = acc + jnp.dot(p.astype(BF16), v_ref[:, hs].astype(BF16), preferred_element_type=F32)
        out = acc / den
        for gi, a in enumerate(heads):
            pieces[a] = out[gi * nq:(gi + 1) * nq, :]
    o_ref[...] = jnp.concatenate(pieces, axis=1)


def _attention(sinks, q, sources, *, grid, nq, q_map, masked=False):
    in_specs = [pl.BlockSpec(memory_space=pltpu.SMEM),
                pl.BlockSpec((nq, D_MODEL), q_map)]
    args = [sinks, q]
    for k_arr, v_arr, b_arr, rows, kv_map, b_block, b_map in sources:
        in_specs += [pl.BlockSpec((rows, KV_WIDTH), kv_map), pl.BlockSpec((rows, KV_WIDTH), kv_map),
                     pl.BlockSpec(b_block, b_map)]
        args += [k_arr, v_arr, b_arr]
    n_out = grid[0] * grid[1] * nq
    return pl.pallas_call(
        functools.partial(_attn_body, n_src=len(sources), nq=nq, masked=masked),
        out_shape=jax.ShapeDtypeStruct((n_out, D_MODEL), F32),
        grid=grid,
        in_specs=in_specs,
        out_specs=pl.BlockSpec((nq, D_MODEL), lambda b, c: (b * grid[1] + c, 0)),
        compiler_params=_params(("parallel", "parallel")),
    )(*args)


def _t5_bucket(rel):
    nb = NUM_BUCKETS // 2
    max_exact = nb // 2
    ret = jnp.where(rel > 0, nb, 0)
    n = jnp.abs(rel)
    large = max_exact + (jnp.log(jnp.maximum(n, 1).astype(F32) / max_exact)
                         / math.log(MAX_DISTANCE / max_exact) * (nb - max_exact)).astype(I32)
    large = jnp.minimum(large, nb - 1)
    return ret + jnp.where(n < max_exact, n, large)


def _grouped_bias(table, q_pos, k_pos):
    bucket = _t5_bucket(k_pos[..., None, :] - q_pos[..., :, None])
    bias = jnp.moveaxis(table[bucket], -1, -3)
    lead = bias.shape[:-3]
    nq, nk = bias.shape[-2:]
    return bias.reshape(lead + (N_KV_HEADS, GROUP * nq, nk))


def _router_body(x_ref, wr_ref, br_ref, idx_ref, wt_ref, rank_ref, cnt_ref, run_ref, *, tm):
    i = pl.program_id(0)

    @pl.when(i == 0)
    def _():
        run_ref[...] = jnp.zeros_like(run_ref)

    logits = jnp.dot(x_ref[...], wr_ref[...], precision=lax.Precision.HIGHEST,
                     preferred_element_type=F32) + br_ref[...]
    lane = lax.broadcasted_iota(I32, (tm, N_EXPERTS), 1)
    work = logits
    vals, sels, hots = [], [], []
    for _ in range(TOP_K):
        m = jnp.max(work, axis=-1, keepdims=True)
        sel = jnp.min(jnp.where(work == m, lane, N_EXPERTS), axis=-1, keepdims=True)
        hot = lane == sel
        vals.append(m)
        sels.append(sel)
        hots.append(hot)
        work = jnp.where(hot, NEG_INF, work)
    exps = [jnp.exp(v - vals[0]) for v in vals]
    den = exps[0] + exps[1] + exps[2] + exps[3]
    any_hot = (hots[0] | hots[1] | hots[2] | hots[3])
    row = lax.broadcasted_iota(I32, (tm, tm), 0)
    col = lax.broadcasted_iota(I32, (tm, tm), 1)
    earlier = (col < row).astype(BF16)
    ahead = jnp.dot(earlier, any_hot.astype(BF16), preferred_element_type=F32) + run_ref[...]
    slot = lax.broadcasted_iota(I32, (tm, TOP_K), 1)
    idx = jnp.zeros((tm, TOP_K), I32)
    wt = jnp.zeros((tm, TOP_K), F32)
    rank = jnp.zeros((tm, TOP_K), F32)
    for k in range(TOP_K):
        idx = jnp.where(slot == k, sels[k], idx)
        wt = jnp.where(slot == k, exps[k] / den, wt)
        rk = jnp.sum(jnp.where(hots[k], ahead, 0.0), axis=-1, keepdims=True)
        rank = jnp.where(slot == k, rk, rank)
    idx_ref[...] = idx
    wt_ref[...] = wt
    rank_ref[...] = rank.astype(I32)
    run_ref[...] += jnp.sum(any_hot.astype(F32), axis=0, keepdims=True)
    cnt_ref[...] = run_ref[...]


def _router(x, w_r, b_r, *, tm):
    t = x.shape[0]
    assert t % tm == 0
    tok = lambda width: pl.BlockSpec((tm, width), lambda i: (i, 0))
    return pl.pallas_call(
        functools.partial(_router_body, tm=tm),
        out_shape=(jax.ShapeDtypeStruct((t, TOP_K), I32), jax.ShapeDtypeStruct((t, TOP_K), F32),
                   jax.ShapeDtypeStruct((t, TOP_K), I32), jax.ShapeDtypeStruct((1, N_EXPERTS), F32)),
        grid=(t // tm,),
        in_specs=[tok(D_MODEL), pl.BlockSpec((D_MODEL, N_EXPERTS), lambda i: (0, 0)),
                  pl.BlockSpec((1, N_EXPERTS), lambda i: (0, 0))],
        out_specs=(tok(TOP_K), tok(TOP_K), tok(TOP_K), pl.BlockSpec((1, N_EXPERTS), lambda i: (0, 0))),
        scratch_shapes=[pltpu.VMEM((1, N_EXPERTS), F32)],
        compiler_params=_params(("arbitrary",)),
    )(x, w_r, b_r.reshape(1, N_EXPERTS))


def _permute_body(*refs, n):
    src_idx, dst_idx, src_hbm = refs[0], refs[1], refs[2]
    dst_hbm, sem = refs[-2], refs[-1]

    def row_copy(s, d):
        return pltpu.make_async_copy(src_hbm.at[pl.ds(s, 1)], dst_hbm.at[pl.ds(d, 1)], sem)

    def issue(a, carry):
        row_copy(src_idx[a], dst_idx[a]).start()
        return carry

    lax.fori_loop(0, n, issue, 0)

    def drain(a, carry):
        row_copy(0, 0).wait()
        return carry

    lax.fori_loop(0, n, drain, 0)


def _permute_rows(src, src_idx, dst_idx, n_dst, *, init=None):
    n = src_idx.shape[0]
    in_specs = [pl.BlockSpec(memory_space=pl.ANY)]
    args = [src_idx, dst_idx, src]
    aliases = {}
    if init is not None:
        in_specs.append(pl.BlockSpec(memory_space=pl.ANY))
        args.append(init)
        aliases = {3: 0}
    return pl.pallas_call(
        functools.partial(_permute_body, n=n),
        out_shape=jax.ShapeDtypeStruct((n_dst, src.shape[1]), src.dtype),
        grid_spec=pltpu.PrefetchScalarGridSpec(
            num_scalar_prefetch=2, grid=(1,), in_specs=in_specs,
            out_specs=pl.BlockSpec(memory_space=pl.ANY),
            scratch_shapes=[pltpu.SemaphoreType.DMA(())]),
        input_output_aliases=aliases,
        compiler_params=_params(("arbitrary",)),
    )(*args)


def _clamped_swiglu_pairs(h):
    glu = jnp.minimum(h, SWIGLU_LIMIT)
    gated = glu * _sigmoid(SWIGLU_ALPHA * glu)
    lin = jnp.clip(h, -SWIGLU_LIMIT, SWIGLU_LIMIT) + 1.0
    width = 2 * LANES
    r = lax.broadcasted_iota(I32, (width, LANES), 0)
    c = lax.broadcasted_iota(I32, (width, LANES), 1)
    pick_even = (r == 2 * c).astype(BF16)
    out = []
    for b in range(h.shape[1] // width):
        halves = []
        for s in range(2):
            lo = b * width + s * LANES
            halves.append(gated[:, lo:lo + LANES] * pltpu.roll(lin[:, lo:lo + LANES], LANES - 1, 1))
        prod = jnp.concatenate(halves, axis=1).astype(BF16)
        out.append(jnp.dot(prod, pick_even, preferred_element_type=F32).astype(BF16))
    return jnp.concatenate(out, axis=1)


def _gmm_body(texp_ref, tsrc_ref, nused_ref, x_ref, w1_ref, b1_ref, w2_ref, b2_ref, o_ref):
    i = pl.program_id(0)
    j = pl.program_id(1)

    @pl.when(i < nused_ref[0])
    def _():
        h = jnp.dot(x_ref[...].astype(BF16), w1_ref[...].astype(BF16),
                    preferred_element_type=F32) + b1_ref[...]
        act = _clamped_swiglu_pairs(h)
        y = jnp.dot(act, w2_ref[...].astype(BF16), preferred_element_type=F32)

        @pl.when(j == 0)
        def _():
            o_ref[...] = y + b2_ref[...]

        @pl.when(j > 0)
        def _():
            o_ref[...] += y


def _grouped_experts(xs, tile_expert, tile_src, n_used, w1, b1, w2, b2, layer, *, tm, dc, vmem_mb=56):
    n_tiles = xs.shape[0] // tm
    d_exp = w2.shape[2]
    nj = d_exp // dc
    b1r = b1.reshape(b1.shape[0], N_EXPERTS, 1, 2 * d_exp)
    b2r = b2.reshape(b2.shape[0], N_EXPERTS, 1, D_MODEL)
    grid_spec = pltpu.PrefetchScalarGridSpec(
        num_scalar_prefetch=3,
        grid=(n_tiles, nj),
        in_specs=[
            pl.BlockSpec((tm, D_MODEL), lambda i, j, te, ts, nu: (ts[i], 0)),
            pl.BlockSpec((None, None, D_MODEL, 2 * dc), lambda i, j, te, ts, nu: (layer, te[i], 0, j)),
            pl.BlockSpec((None, None, 1, 2 * dc), lambda i, j, te, ts, nu: (layer, te[i], 0, j)),
            pl.BlockSpec((None, None, dc, D_MODEL), lambda i, j, te, ts, nu: (layer, te[i], j, 0)),
            pl.BlockSpec((None, None, 1, D_MODEL), lambda i, j, te, ts, nu: (layer, te[i], 0, 0)),
        ],
        out_specs=pl.BlockSpec((tm, D_MODEL), lambda i, j, te, ts, nu: (ts[i], 0)),
    )
    return pl.pallas_call(
        _gmm_body,
        out_shape=jax.ShapeDtypeStruct(xs.shape, F32),
        grid_spec=grid_spec,
        compiler_params=_params(("arbitrary", "arbitrary"), vmem_mb),
    )(tile_expert, tile_src, n_used, xs, w1, b1r, w2, b2r)


def _combine_body(x_ref, y_ref, wt_ref, g_ref, b_ref, o_ref):
    wt = wt_ref[...]
    moe = wt[:, 0:1] * y_ref[0]
    for k in range(1, TOP_K):
        moe = moe + wt[:, k:k + 1] * y_ref[k]
    o_ref[...] = _layer_norm_rows(DEEPNORM_ALPHA * x_ref[...] + moe, g_ref[...], b_ref[...])


def _combine(x, y_slots, wt, g, b, *, tm):
    t = x.shape[0]
    tok = pl.BlockSpec((tm, D_MODEL), lambda i: (i, 0))
    vec = pl.BlockSpec((1, D_MODEL), lambda i: (0, 0))
    return pl.pallas_call(
        _combine_body,
        out_shape=jax.ShapeDtypeStruct((t, D_MODEL), F32),
        grid=(t // tm,),
        in_specs=[tok, pl.BlockSpec((TOP_K, tm, D_MODEL), lambda i: (0, i, 0)),
                  pl.BlockSpec((tm, TOP_K), lambda i: (i, 0)), vec, vec],
        out_specs=tok,
        compiler_params=_params(("parallel",), 48),
    )(x, y_slots, wt, g.reshape(1, D_MODEL), b.reshape(1, D_MODEL))


def _moe_layer(x, layer, w_r, b_r, w1, b1, w2, b2, ln_g, ln_b, *, tm_router, tm_combine, tm_exp, dc):
    t = x.shape[0]
    idx, wt, rank, counts = _router(x, w_r[layer], b_r[layer], tm=tm_router)
    counts = counts[0].astype(I32)
    padded = (counts + tm_exp - 1) // tm_exp * tm_exp
    ends = jnp.cumsum(padded)
    pos = ((ends - padded)[idx] + rank).reshape(-1)
    n_tiles = (t * TOP_K) // tm_exp + N_EXPERTS
    n_used = ends[-1] // tm_exp
    tile_src = jnp.minimum(jnp.arange(n_tiles, dtype=I32), n_used - 1)
    tile_expert = jnp.minimum(
        jnp.sum((tile_src[:, None] * tm_exp >= ends[None, :]).astype(I32), axis=1), N_EXPERTS - 1)
    flat = jnp.arange(t * TOP_K, dtype=I32)
    xs = _permute_rows(x, flat // TOP_K, pos, n_tiles * tm_exp,
                       init=jnp.zeros((n_tiles * tm_exp, D_MODEL), F32))
    ys = _grouped_experts(xs, tile_expert, tile_src, n_used.reshape(1), w1, b1, w2, b2, layer,
                          tm=tm_exp, dc=dc)
    y_slots = _permute_rows(ys, pos, (flat % TOP_K) * t + flat // TOP_K, TOP_K * t)
    return _combine(x, y_slots.reshape(TOP_K, t, D_MODEL), wt, ln_g, ln_b, tm=tm_combine)


def kernel(x_prompt, x_sample, state_hgrn, cache_meta_k, cache_meta_v, cache_win_k, cache_win_v, meta_tokens, a_w_in, a_lb_logits, a_gnorm, a_w_out, kv_w, kv_b, b_w_q, b_b_q, b_sinks, b_w_o, b_b_o, rel_bias, ln1_g, ln1_b, ln2_g, ln2_b, moe_w_r, moe_b_r, moe_w1, moe_b1, moe_w2, moe_b2):
    bp, seq = x_prompt.shape[:2]
    bs, dseq = x_sample.shape[:2]
    assert a_w_in.shape[0] == 1 and b_w_q.shape[0] == 1 and dseq == N_META
    n_real = bp * seq
    n_meta = bp * N_META
    n_samp = bs * dseq
    t = n_real + n_meta + n_samp
    tm_tok = t // 10
    assert tm_tok * 10 == t and tm_tok % SUBLANES == 0
    nc = seq // CHUNK
    tm_ln = tm_tok // 2
    moe_cfg = dict(tm_router=tm_tok, tm_combine=160, tm_exp=512, dc=512)

    h = jnp.concatenate([x_prompt.reshape(n_real, D_MODEL),
                         jnp.tile(meta_tokens.astype(x_prompt.dtype), (bp, 1)),
                         x_sample.reshape(n_samp, D_MODEL)], axis=0)

    lower = jnp.cumsum(jax.nn.softmax(a_lb_logits.astype(F32), axis=0), axis=0)[0].reshape(1, D_MODEL)
    gnorm = a_gnorm[0].astype(F32).reshape(1, D_MODEL)
    proj = _matmul(h, a_w_in[0], tm=tm_tok, tn=2048, tk=512)
    n_short = bp + bs
    s0_short = jnp.concatenate([jnp.zeros((bp,) + state_hgrn.shape[2:], F32), state_hgrn[0].astype(F32)], axis=0)
    o_short, s_short = _hgrn(proj, lower, gnorm, s0_short, n_seq=n_short, seq_len=N_META,
                             row0=n_real, c=N_META, cps=1, hb=2)
    o_real, s_prompt = _hgrn(proj, lower, gnorm, s_short[:bp], n_seq=bp, seq_len=seq,
                             row0=0, c=CHUNK, cps=4, hb=2)
    o_gated = jnp.concatenate([o_real, o_short], axis=0)
    h = _matmul(o_gated, a_w_out[0], ln=(h, ln1_g[0], ln1_b[0]), tm=tm_ln, tn=D_MODEL, tk=512)
    h = _moe_layer(h, 0, moe_w_r, moe_b_r, moe_w1, moe_b1, moe_w2, moe_b2, ln2_g[0], ln2_b[0], **moe_cfg)

    kv = _matmul(h, kv_w, bias=kv_b, tm=tm_tok, tn=2 * KV_WIDTH, tk=512)
    k_all, v_all = kv[:, :KV_WIDTH], kv[:, KV_WIDTH:]
    q_all = _matmul(h, b_w_q[0], bias=b_b_q[0], tm=tm_tok, tn=D_MODEL, tk=512)
    sinks = b_sinks[0].astype(F32)
    table = rel_bias.astype(F32)

    meta_blk = n_real // N_META
    samp_blk = (n_real + n_meta) // N_META
    ql = jnp.arange(CHUNK)
    frame_sources = []
    for w in range(WINDOW_CHUNKS + 1):
        off = WINDOW_CHUNKS - w
        bias_w = _grouped_bias(table, ql, ql - off * CHUNK)
        frame_sources.append((k_all, v_all, bias_w, CHUNK,
                              lambda b, c, off=off: (b * nc + jnp.maximum(c - off, 0), 0),
                              bias_w.shape, lambda b, c: (0, 0, 0)))
    q_pos = N_META + jnp.arange(nc)[:, None] * CHUNK + ql[None]
    bias_m = _grouped_bias(table, q_pos, jnp.broadcast_to(jnp.arange(N_META), (nc, N_META)))
    meta_source = (k_all, v_all, bias_m, N_META, lambda b, c: (meta_blk + b, 0),
                   (None,) + bias_m.shape[1:], lambda b, c: (c, 0, 0, 0))
    attn_real = _attention(sinks, q_all, frame_sources[:WINDOW_CHUNKS] + [frame_sources[-1], meta_source],
                           grid=(bp, nc), nq=CHUNK, q_map=lambda b, c: (b * nc + c, 0), masked=True)
    pm = jnp.arange(N_META)
    bias_mm = _grouped_bias(table, pm, pm)
    attn_meta = _attention(sinks, q_all,
                           [(k_all, v_all, bias_mm, N_META, lambda b, c: (meta_blk + b, 0),
                             bias_mm.shape, lambda b, c: (0, 0, 0))],
                           grid=(bp, 1), nq=N_META, q_map=lambda b, c: (meta_blk + b, 0))
    win = cache_win_k.shape[1]
    qs_pos = N_META + PAST_LEN + jnp.arange(dseq)
    bias_sm = _grouped_bias(table, qs_pos, pm)
    bias_sw = _grouped_bias(table, qs_pos, N_META + PAST_LEN - win + jnp.arange(win))
    bias_sn = _grouped_bias(table, qs_pos, qs_pos)
    flat_kv = lambda a: a.reshape(-1, KV_WIDTH).astype(F32)
    const3 = lambda b, c: (0, 0, 0)
    attn_samp = _attention(
        sinks, q_all,
        [(flat_kv(cache_meta_k), flat_kv(cache_meta_v), bias_sm, N_META, lambda b, c: (b, 0), bias_sm.shape, const3),
         (flat_kv(cache_win_k), flat_kv(cache_win_v), bias_sw, win, lambda b, c: (b, 0), bias_sw.shape, const3),
         (k_all, v_all, bias_sn, dseq, lambda b, c: (samp_blk + b, 0), bias_sn.shape, const3)],
        grid=(bs, 1), nq=dseq, q_map=lambda b, c: (samp_blk + b, 0))
    attn = jnp.concatenate([attn_real, attn_meta, attn_samp], axis=0)
    h = _matmul(attn, b_w_o[0], bias=b_b_o[0], ln=(h, ln1_g[1], ln1_b[1]), tm=tm_ln, tn=D_MODEL, tk=512)
    h = _moe_layer(h, 1, moe_w_r, moe_b_r, moe_w1, moe_b1, moe_w2, moe_b2, ln2_g[1], ln2_b[1], **moe_cfg)

    heads = lambda a, n: a.reshape(n, -1, N_KV_HEADS, HEAD_DIM)
    k_real, v_real = heads(k_all[:n_real], bp), heads(v_all[:n_real], bp)
    k_meta, v_meta = heads(k_all[n_real:n_real + n_meta], bp), heads(v_all[n_real:n_real + n_meta], bp)
    k_samp, v_samp = heads(k_all[n_real + n_meta:], bs), heads(v_all[n_real + n_meta:], bs)
    win_p = min(WINDOW_CHUNKS * CHUNK, seq)
    return (h[:n_real].reshape(bp, seq, D_MODEL),
            h[n_real + n_meta:].reshape(bs, dseq, D_MODEL),
            s_prompt[None],
            k_meta, v_meta,
            k_real[:, seq - win_p:], v_real[:, seq - win_p:],
            s_short[bp:][None],
            k_samp, v_samp)
```
